```python
import math
import jax, jax.numpy as jnp
from jax import lax
import numpy as np

D_MODEL = 4096
BATCH = 4
SEQ = 2048
DEPTH = 1
DEC_BATCH = 128
DEC_SEQ = 8
PAST_LEN = 16384
PAGE_SIZE = 128

D_MIX = D_MODEL
D_S5 = D_MIX // 2
S5_H = 16
S5_G = D_S5 // S5_H
S5_P = 64
D_SSD = D_MIX - D_S5
SSD_HEADDIM = 64
SSD_HEADS = D_SSD // SSD_HEADDIM
SSD_GROUPS = 8
SSD_STATE = 128
SSD_CONV = 4
SSD_CHUNK = 128
D_XBC = D_SSD + 2 * SSD_GROUPS * SSD_STATE
D_IN = D_S5 + D_SSD + D_XBC + SSD_HEADS
D_FF = ((8 * D_MODEL // 3 + 255) // 256) * 256
FFN_CONV = 3
EPS = 1e-6

kernel_name = "hymba_s5_ssd_convffn_step"


def rmsnorm(x, g):
    xf = x.astype(jnp.float32)
    y = xf * lax.rsqrt(jnp.mean(xf * xf, axis=-1, keepdims=True) + EPS) * g.astype(jnp.float32)
    return y.astype(x.dtype)


def causal_dwconv(x, buf, w, b):
    K = w.shape[0]
    L = x.shape[1]
    xp = jnp.concatenate([buf.astype(x.dtype), x], axis=1)
    out = b + sum(xp[:, k:k + L] * w[k] for k in range(K))
    return out, xp[:, L:]


def _complex_affine_combine(e1, e2):
    a1r, a1i, b1r, b1i = e1
    a2r, a2i, b2r, b2i = e2
    ar = a1r * a2r - a1i * a2i
    ai = a1r * a2i + a1i * a2r
    br = a2r * b1r - a2i * b1i + b2r
    bi = a2r * b1i + a2i * b1r + b2i
    return (ar, ai, br, bi)


def s5_mixer(u, s_re0, s_im0, prm):
    b, L, _ = u.shape
    uf = u.astype(jnp.float32).reshape(b, L, S5_G, S5_H)
    ar = prm['s5_a_re'].astype(jnp.float32)
    ai = prm['s5_a_im'].astype(jnp.float32)
    dt = jnp.exp(prm['s5_log_dt'].astype(jnp.float32))[:, None]
    mag = jnp.exp(ar * dt)
    lbr = mag * jnp.cos(ai * dt)
    lbi = mag * jnp.sin(ai * dt)
    den = ar * ar + ai * ai
    nr = lbr - 1.0
    fr = (nr * ar + lbi * ai) / den
    fi = (lbi * ar - nr * ai) / den
    b_re = prm['s5_b_re'].astype(jnp.float32)
    b_im = prm['s5_b_im'].astype(jnp.float32)
    bbr = fr[..., None] * b_re - fi[..., None] * b_im
    bbi = fr[..., None] * b_im + fi[..., None] * b_re
    bur = jnp.einsum('gph,blgh->blgp', bbr, uf)
    bui = jnp.einsum('gph,blgh->blgp', bbi, uf)
    s_re0 = s_re0.astype(jnp.float32)
    s_im0 = s_im0.astype(jnp.float32)
    bur = bur.at[:, 0].add(lbr * s_re0 - lbi * s_im0)
    bui = bui.at[:, 0].add(lbr * s_im0 + lbi * s_re0)
    a_r = jnp.broadcast_to(lbr, bur.shape)
    a_i = jnp.broadcast_to(lbi, bui.shape)
    _, _, sr, si = lax.associative_scan(_complex_affine_combine, (a_r, a_i, bur, bui), axis=1)
    y = (jnp.einsum('ghp,blgp->blgh', prm['s5_c_re'].astype(jnp.float32), sr)
         - jnp.einsum('ghp,blgp->blgh', prm['s5_c_im'].astype(jnp.float32), si)
         + prm['s5_d'].astype(jnp.float32) * uf)
    y = y.reshape(b, L, D_S5)
    g = jax.nn.gelu(y)
    out = g * jax.nn.sigmoid(g @ prm['s5_w_glu'].astype(jnp.float32) + prm['s5_b_glu'].astype(jnp.float32))
    return out, sr[:, -1], si[:, -1]


def ssd_scan(x, dt, A, Bm, Cm, h0):
    b, L, H, P = x.shape
    rep = H // Bm.shape[2]
    chunk = math.gcd(L, SSD_CHUNK)
    c = L // chunk
    Bh = jnp.repeat(Bm, rep, axis=2).reshape(b, c, chunk, H, -1)
    Ch = jnp.repeat(Cm, rep, axis=2).reshape(b, c, chunk, H, -1)
    xd = (x * dt[..., None]).reshape(b, c, chunk, H, P)
    a = (dt * A).reshape(b, c, chunk, H).transpose(0, 3, 1, 2)
    acs = jnp.cumsum(a, axis=-1)
    mask = jnp.tril(jnp.ones((chunk, chunk), dtype=bool))
    seg = acs[..., :, None] - acs[..., None, :]
    Lmat = jnp.exp(jnp.where(mask, seg, -jnp.inf))
    cb = jnp.einsum('bclhn,bcshn->bhcls', Ch, Bh)
    y_diag = jnp.einsum('bhcls,bcshp->bclhp', cb * Lmat, xd)
    decay_states = jnp.exp(acs[..., -1:] - acs)
    states = jnp.einsum('bcshn,bhcs,bcshp->bchpn', Bh, decay_states, xd)
    chunk_decay = jnp.exp(acs[..., -1])

    def step(h, inp):
        st, dec = inp
        return dec[..., None, None] * h + st, h

    h_final, h_prev = lax.scan(step, h0.astype(jnp.float32),
                               (states.transpose(1, 0, 2, 3, 4), chunk_decay.transpose(2, 0, 1)))
    h_prev = h_prev.transpose(1, 0, 2, 3, 4)
    y_off = jnp.einsum('bclhn,bchpn,bhcl->bclhp', Ch, h_prev, jnp.exp(acs))
    return (y_diag + y_off).reshape(b, L, H, P), h_final


def ssd_mixer(z, xbc, dt_raw, conv0, h0, prm):
    b, L, _ = z.shape
    xbc_c, conv_new = causal_dwconv(xbc, conv0, prm['ssd_conv_w'], prm['ssd_conv_b'])
    xbc_c = jax.nn.silu(xbc_c.astype(jnp.float32))
    gn = SSD_GROUPS * SSD_STATE
    xs = xbc_c[..., :D_SSD].reshape(b, L, SSD_HEADS, SSD_HEADDIM)
    Bm = xbc_c[..., D_SSD:D_SSD + gn].reshape(b, L, SSD_GROUPS, SSD_STATE)
    Cm = xbc_c[..., D_SSD + gn:].reshape(b, L, SSD_GROUPS, SSD_STATE)
    dt = jax.nn.softplus(dt_raw.astype(jnp.float32) + prm['ssd_dt_bias'].astype(jnp.float32))
    A = -jnp.exp(prm['ssd_a_log'].astype(jnp.float32))
    y, h_new = ssd_scan(xs, dt, A, Bm, Cm, h0)
    y = y + prm['ssd_d'].astype(jnp.float32)[:, None] * xs
    y = y.reshape(b, L, D_SSD) * jax.nn.silu(z.astype(jnp.float32))
    yg = y.reshape(b, L, SSD_GROUPS, D_SSD // SSD_GROUPS)
    yg = yg * lax.rsqrt(jnp.mean(yg * yg, axis=-1, keepdims=True) + EPS)
    y = yg.reshape(b, L, D_SSD) * prm['ssd_norm_w'].astype(jnp.float32)
    return y, conv_new, h_new


def layer_step(x, s5_re0, s5_im0, ssd_conv0, ssd_h0, ffn_conv0, prm):
    h = rmsnorm(x, prm['g_mix_pre'])
    proj = h @ prm['w_in']
    u = proj[..., :D_S5]
    z = proj[..., D_S5:D_S5 + D_SSD]
    xbc = proj[..., D_S5 + D_SSD:D_S5 + D_SSD + D_XBC]
    dt_raw = proj[..., D_S5 + D_SSD + D_XBC:]
    y_a, s5_re, s5_im = s5_mixer(u, s5_re0, s5_im0, prm)
    y_b, conv_new, h_new = ssd_mixer(z, xbc, dt_raw, ssd_conv0, ssd_h0, prm)
    mix = jnp.concatenate([y_a, y_b], axis=-1).astype(x.dtype) @ prm['w_out']
    x = x + rmsnorm(mix, prm['g_mix_post'])
    hf = rmsnorm(x, prm['g_ffn_pre'])
    up = hf @ prm['ffn_w_up']
    up_c, ffn_conv_new = causal_dwconv(up, ffn_conv0, prm['ffn_conv_w'], prm['ffn_conv_b'])
    gate, val = up_c[..., :D_FF], up_c[..., D_FF:]
    ff = (jax.nn.gelu(gate, approximate=True) * val) @ prm['ffn_w_down']
    x = x + rmsnorm(ff, prm['g_ffn_post'])
    return x, (s5_re, s5_im, conv_new, h_new, ffn_conv_new)


def setup_inputs(seed: int = 0) -> dict:
    key = jax.random.key(seed)
    nk = iter(jax.random.split(key, 40))
    f32 = jnp.float32

    def nrm(shape, scale=1.0):
        return jax.random.normal(next(nk), shape, f32) * scale

    Dp = (DEPTH,)
    dt0 = jnp.exp(jax.random.uniform(next(nk), Dp + (SSD_HEADS,), f32, math.log(1e-3), math.log(1e-1)))
    return {
        "x_prompt": nrm((BATCH, SEQ, D_MODEL)),
        "x_sample": nrm((DEC_BATCH, DEC_SEQ, D_MODEL)),
        "state_s5_re": nrm(Dp + (DEC_BATCH, S5_G, S5_P), 0.5),
        "state_s5_im": nrm(Dp + (DEC_BATCH, S5_G, S5_P), 0.5),
        "state_ssd_conv": nrm(Dp + (DEC_BATCH, SSD_CONV - 1, D_XBC)),
        "state_ssd": nrm(Dp + (DEC_BATCH, SSD_HEADS, SSD_HEADDIM, SSD_STATE), 0.1),
        "state_ffn_conv": nrm(Dp + (DEC_BATCH, FFN_CONV - 1, 2 * D_FF)),
        "g_mix_pre": 1.0 + nrm(Dp + (D_MODEL,), 0.02),
        "g_mix_post": 1.0 + nrm(Dp + (D_MODEL,), 0.02),
        "g_ffn_pre": 1.0 + nrm(Dp + (D_MODEL,), 0.02),
        "g_ffn_post": 1.0 + nrm(Dp + (D_MODEL,), 0.02),
        "w_in": nrm(Dp + (D_MODEL, D_IN), D_MODEL ** -0.5),
        "s5_a_re": -0.5 + nrm(Dp + (S5_G, S5_P), 0.01),
        "s5_a_im": jnp.pi * jnp.arange(S5_P, dtype=f32) + nrm(Dp + (S5_G, S5_P), 0.01),
        "s5_log_dt": jax.random.uniform(next(nk), Dp + (S5_G,), f32, math.log(1e-3), math.log(1e-1)),
        "s5_b_re": nrm(Dp + (S5_G, S5_P, S5_H), (2 * S5_H) ** -0.5),
        "s5_b_im": nrm(Dp + (S5_G, S5_P, S5_H), (2 * S5_H) ** -0.5),
        "s5_c_re": nrm(Dp + (S5_G, S5_H, S5_P), (2 * S5_P) ** -0.5),
        "s5_c_im": nrm(Dp + (S5_G, S5_H, S5_P), (2 * S5_P) ** -0.5),
        "s5_d": nrm(Dp + (S5_G, S5_H)),
        "s5_w_glu": nrm(Dp + (D_S5, D_S5), D_S5 ** -0.5),
        "s5_b_glu": nrm(Dp + (D_S5,), 0.01),
        "ssd_conv_w": nrm(Dp + (SSD_CONV, D_XBC), SSD_CONV ** -0.5),
        "ssd_conv_b": nrm(Dp + (D_XBC,), 0.01),
        "ssd_dt_bias": dt0 + jnp.log(-jnp.expm1(-dt0)),
        "ssd_a_log": jnp.log(jax.random.uniform(next(nk), Dp + (SSD_HEADS,), f32, 1.0, 16.0)),
        "ssd_d": 1.0 + nrm(Dp + (SSD_HEADS,), 0.1),
        "ssd_norm_w": 1.0 + nrm(Dp + (D_SSD,), 0.02),
        "w_out": nrm(Dp + (D_MIX, D_MODEL), D_MIX ** -0.5),
        "ffn_w_up": nrm(Dp + (D_MODEL, 2 * D_FF), D_MODEL ** -0.5),
        "ffn_conv_w": nrm(Dp + (FFN_CONV, 2 * D_FF), FFN_CONV ** -0.5),
        "ffn_conv_b": nrm(Dp + (2 * D_FF,), 0.01),
        "ffn_w_down": nrm(Dp + (D_FF, D_MODEL), D_FF ** -0.5),
    }


def reference(x_prompt, x_sample, state_s5_re, state_s5_im, state_ssd_conv, state_ssd, state_ffn_conv,
              g_mix_pre, g_mix_post, g_ffn_pre, g_ffn_post, w_in,
              s5_a_re, s5_a_im, s5_log_dt, s5_b_re, s5_b_im, s5_c_re, s5_c_im, s5_d, s5_w_glu, s5_b_glu,
              ssd_conv_w, ssd_conv_b, ssd_dt_bias, ssd_a_log, ssd_d, ssd_norm_w,
              w_out, ffn_w_up, ffn_conv_w, ffn_conv_b, ffn_w_down):
    weights = dict(g_mix_pre=g_mix_pre, g_mix_post=g_mix_post, g_ffn_pre=g_ffn_pre, g_ffn_post=g_ffn_post,
                   w_in=w_in, s5_a_re=s5_a_re, s5_a_im=s5_a_im, s5_log_dt=s5_log_dt,
                   s5_b_re=s5_b_re, s5_b_im=s5_b_im, s5_c_re=s5_c_re, s5_c_im=s5_c_im, s5_d=s5_d,
                   s5_w_glu=s5_w_glu, s5_b_glu=s5_b_glu, ssd_conv_w=ssd_conv_w, ssd_conv_b=ssd_conv_b,
                   ssd_dt_bias=ssd_dt_bias, ssd_a_log=ssd_a_log, ssd_d=ssd_d, ssd_norm_w=ssd_norm_w,
                   w_out=w_out, ffn_w_up=ffn_w_up, ffn_conv_w=ffn_conv_w, ffn_conv_b=ffn_conv_b,
                   ffn_w_down=ffn_w_down)
    bp = x_prompt.shape[0]
    xp, xs = x_prompt, x_sample
    p_states, s_states = [], []
    for l in range(DEPTH):
        prm = {k: v[l] for k, v in weights.items()}
        xp, st_p = layer_step(
            xp,
            jnp.zeros((bp, S5_G, S5_P), jnp.float32),
            jnp.zeros((bp, S5_G, S5_P), jnp.float32),
            jnp.zeros((bp, SSD_CONV - 1, D_XBC), xp.dtype),
            jnp.zeros((bp, SSD_HEADS, SSD_HEADDIM, SSD_STATE), jnp.float32),
            jnp.zeros((bp, FFN_CONV - 1, 2 * D_FF), xp.dtype),
            prm)
        xs, st_s = layer_step(xs, state_s5_re[l], state_s5_im[l], state_ssd_conv[l], state_ssd[l],
                              state_ffn_conv[l], prm)
        p_states.append(st_p)
        s_states.append(st_s)
    p_s5_re, p_s5_im, p_ssd_conv, p_ssd, p_ffn_conv = [jnp.stack([st[i] for st in p_states]) for i in range(5)]
    s_s5_re, s_s5_im, s_ssd_conv, s_ssd, s_ffn_conv = [jnp.stack([st[i] for st in s_states]) for i in range(5)]
    return (xp, xs, p_s5_re, p_s5_im, p_ssd_conv, p_ssd, p_ffn_conv,
            s_s5_re, s_s5_im, s_ssd_conv, s_ssd, s_ffn_conv)
```

```python
import functools
import math

import jax
import jax.numpy as jnp
from jax import lax
from jax.experimental import pallas as pl
from jax.experimental.pallas import tpu as pltpu

D_MODEL = 4096
D_S5 = 2048
S5_H = 16
S5_G = 128
S5_P = 64
D_SSD = 2048
SSD_HEADDIM = 64
SSD_HEADS = 32
SSD_GROUPS = 8
SSD_STATE = 128
SSD_CONV = 4
D_XBC = 4096
D_PROJ = D_S5 + D_SSD + D_XBC
D_FF = 11008
FFN_CONV = 3
EPS = 1e-6

F32 = jnp.float32
BF16 = jnp.bfloat16

SUBLANES = 8
VMEM_LIMIT = 56 * 1024 * 1024

S5_NCH = 8
S5_CU = D_S5 // S5_NCH
S5_CS = S5_G * S5_P // S5_NCH
SSD_GW = D_SSD // SSD_GROUPS
SSD_R = 128


def _cparams(sem):
    return pltpu.CompilerParams(dimension_semantics=sem, vmem_limit_bytes=VMEM_LIMIT)


def _rms(x):
    return x * lax.rsqrt(jnp.mean(x * x, axis=-1, keepdims=True) + EPS)


def _dot(a, b):
    return jnp.dot(a, b, preferred_element_type=F32)


def _dot_nt(a, b):
    return lax.dot_general(a, b, (((1,), (1,)), ((), ())), preferred_element_type=F32)


def _const_spec(shape):
    nd = len(shape)
    return pl.BlockSpec(shape, lambda *_: (0,) * nd, pipeline_mode=pl.Buffered(1))


def _inproj_kernel(x_ref, g_ref, w_ref, wdt_ref, o_ref, dt_ref, h_scr):
    @pl.when(pl.program_id(1) == 0)
    def _():
        hb = (_rms(x_ref[...]) * g_ref[...]).astype(BF16)
        h_scr[...] = hb
        dt_ref[...] = _dot(hb, wdt_ref[...])

    o_ref[...] = _dot(h_scr[...], w_ref[...])


def _inproj(x2d, g, w_in_bf, wdt_bf, tm=512, tn=512):
    m = x2d.shape[0]
    return pl.pallas_call(
        _inproj_kernel,
        grid=(m // tm, D_PROJ // tn),
        in_specs=[
            pl.BlockSpec((tm, D_MODEL), lambda i, j: (i, 0)),
            pl.BlockSpec((1, D_MODEL), lambda i, j: (0, 0)),
            pl.BlockSpec((D_MODEL, tn), lambda i, j: (0, j)),
            pl.BlockSpec((D_MODEL, SSD_HEADS), lambda i, j: (0, 0)),
        ],
        out_specs=[
            pl.BlockSpec((tm, tn), lambda i, j: (i, j)),
            pl.BlockSpec((tm, SSD_HEADS), lambda i, j: (i, 0)),
        ],
        out_shape=[jax.ShapeDtypeStruct((m, D_PROJ), F32), jax.ShapeDtypeStruct((m, SSD_HEADS), F32)],
        scratch_shapes=[pltpu.VMEM((tm, D_MODEL), BF16)],
        compiler_params=_cparams(("arbitrary", "arbitrary")),
        name="inproj",
    )(x2d, g, w_in_bf, wdt_bf)


def _s5_kernel(*refs, rows, prompt):
    if prompt:
        (u_ref, wbr_ref, wbi_ref, wcr_ref, wci_ref, tab_ref, d_ref, wglu_ref, bglu_ref,
         y_ref, sro_ref, sio_ref, sr_scr, si_scr, g_scr, car_scr) = refs

        @pl.when(pl.program_id(1) == 0)
        def _():
            car_scr[...] = jnp.zeros_like(car_scr)
    else:
        (u_ref, s0r_ref, s0i_ref, wbr_ref, wbi_ref, wcr_ref, wci_ref, tab_ref, d_ref, wglu_ref, bglu_ref,
         y_ref, sro_ref, sio_ref, sr_scr, si_scr, g_scr) = refs

    for c in range(S5_NCH):
        lanes = slice(c * S5_CS, (c + 1) * S5_CS)
        ulanes = slice(c * S5_CU, (c + 1) * S5_CU)
        u = u_ref[:, ulanes]
        ub = u.astype(BF16)
        sr_scr[...] = _dot(ub, wbr_ref[c])
        si_scr[...] = _dot(ub, wbi_ref[c])

        def tile(i, carry, lanes=lanes):
            r0 = pl.multiple_of(i * SUBLANES, SUBLANES)
            xr = sr_scr[pl.ds(r0, SUBLANES), :]
            xi = si_scr[pl.ds(r0, SUBLANES), :]
            for k, d in enumerate((1, 2, 4)):
                ar = tab_ref[2 * k, :, lanes]
                ai = tab_ref[2 * k + 1, :, lanes]
                rr = pltpu.roll(xr, d, axis=0)
                ri = pltpu.roll(xi, d, axis=0)
                xr, xi = xr + ar * rr - ai * ri, xi + ar * ri + ai * rr
            pr = tab_ref[6, :, lanes]
            pi = tab_ref[7, :, lanes]
            if prompt:
                cr, ci = carry
            else:
                cr = s0r_ref[pl.ds(i, 1), lanes]
                ci = s0i_ref[pl.ds(i, 1), lanes]
            xr, xi = xr + pr * cr - pi * ci, xi + pr * ci + pi * cr
            sr_scr[pl.ds(r0, SUBLANES), :] = xr
            si_scr[pl.ds(r0, SUBLANES), :] = xi
            lr = xr[SUBLANES - 1:SUBLANES, :]
            li = xi[SUBLANES - 1:SUBLANES, :]
            if prompt:
                return lr, li
            sro_ref[pl.ds(i, 1), lanes] = lr
            sio_ref[pl.ds(i, 1), lanes] = li
            return carry

        if prompt:
            fr, fi = lax.fori_loop(0, rows // SUBLANES, tile, (car_scr[0:1, lanes], car_scr[1:2, lanes]))
            car_scr[0:1, lanes] = fr
            car_scr[1:2, lanes] = fi
            sro_ref[:, lanes] = fr
            sio_ref[:, lanes] = fi
        else:
            lax.fori_loop(0, rows // SUBLANES, tile, 0)

        yc = (_dot(sr_scr[...].astype(BF16), wcr_ref[c]) + _dot(si_scr[...].astype(BF16), wci_ref[c])
              + d_ref[:, ulanes] * u)
        g_scr[:, ulanes] = jax.nn.gelu(yc)

    g = g_scr[...]
    gl = _dot(g.astype(BF16), wglu_ref[...]) + bglu_ref[...]
    y_ref[...] = (g * jax.nn.sigmoid(gl)).astype(BF16)


def _s5(proj, s0r, s0i, w, *, batch, seqlen, rows=256):
    m = proj.shape[0]
    prompt = s0r is None
    nstate = S5_G * S5_P
    wspecs = [
        _const_spec((S5_NCH, S5_CU, S5_CS)), _const_spec((S5_NCH, S5_CU, S5_CS)),
        _const_spec((S5_NCH, S5_CS, S5_CU)), _const_spec((S5_NCH, S5_CS, S5_CU)),
        _const_spec((8, SUBLANES, nstate)), _const_spec((1, D_S5)),
        _const_spec((D_S5, D_S5)), _const_spec((1, D_S5)),
    ]
    wargs = (w["wbr"], w["wbi"], w["wcr"], w["wci"], w["tab"], w["d"], w["wglu"], w["bglu"])
    scratch = [pltpu.VMEM((rows, S5_CS), F32), pltpu.VMEM((rows, S5_CS), F32), pltpu.VMEM((rows, D_S5), F32)]
    if prompt:
        nc = seqlen // rows
        grid = (batch, nc)
        in_specs = [pl.BlockSpec((rows, D_S5), lambda b, c: (b * nc + c, 0))] + wspecs
        out_specs = [
            pl.BlockSpec((rows, D_S5), lambda b, c: (b * nc + c, 0)),
            pl.BlockSpec((None, 1, nstate), lambda b, c: (b, 0, 0)),
            pl.BlockSpec((None, 1, nstate), lambda b, c: (b, 0, 0)),
        ]
        out_shape = [jax.ShapeDtypeStruct((m, D_S5), BF16),
                     jax.ShapeDtypeStruct((batch, 1, nstate), F32), jax.ShapeDtypeStruct((batch, 1, nstate), F32)]
        args = (proj,) + wargs
        scratch = scratch + [pltpu.VMEM((SUBLANES, nstate), F32)]
        sem = ("arbitrary", "arbitrary")
    else:
        assert seqlen == SUBLANES
        nseq = rows // seqlen
        grid = (m // rows,)
        in_specs = [pl.BlockSpec((rows, D_S5), lambda i: (i, 0)),
                    pl.BlockSpec((nseq, nstate), lambda i: (i, 0)),
                    pl.BlockSpec((nseq, nstate), lambda i: (i, 0))] + wspecs
        out_specs = [pl.BlockSpec((rows, D_S5), lambda i: (i, 0)),
                     pl.BlockSpec((nseq, nstate), lambda i: (i, 0)),
                     pl.BlockSpec((nseq, nstate), lambda i: (i, 0))]
        out_shape = [jax.ShapeDtypeStruct((m, D_S5), BF16),
                     jax.ShapeDtypeStruct((batch, nstate), F32), jax.ShapeDtypeStruct((batch, nstate), F32)]
        args = (proj, s0r, s0i) + wargs
        sem = ("arbitrary",)
    return pl.pallas_call(
        functools.partial(_s5_kernel, rows=rows, prompt=prompt),
        grid=grid, in_specs=in_specs, out_specs=out_specs, out_shape=out_shape,
        scratch_shapes=scratch, compiler_params=_cparams(sem),
        name="s5_prompt" if prompt else "s5_sample",
    )(*args)


def _s5_weights(a_re, a_im, log_dt, b_re, b_im, c_re, c_im, d, w_glu, b_glu):
    ar = a_re.astype(F32)
    ai = a_im.astype(F32)
    dt = jnp.exp(log_dt.astype(F32))[:, None]
    mag = jnp.exp(ar * dt)
    lbr = mag * jnp.cos(ai * dt)
    lbi = mag * jnp.sin(ai * dt)
    den = ar * ar + ai * ai
    nr = lbr - 1.0
    fr = (nr * ar + lbi * ai) / den
    fi = (lbi * ar - nr * ai) / den
    b_re = b_re.astype(F32)
    b_im = b_im.astype(F32)
    bbr = fr[..., None] * b_re - fi[..., None] * b_im
    bbi = fr[..., None] * b_im + fi[..., None] * b_re
    gl = S5_G // S5_NCH
    eye = jnp.eye(gl, dtype=F32)

    def pack_b(bb):
        t = bb.reshape(S5_NCH, gl, S5_P, S5_H).transpose(0, 1, 3, 2)
        t = t[:, :, :, None, :] * eye[None, :, None, :, None]
        return t.reshape(S5_NCH, S5_CU, S5_CS).astype(BF16)

    def pack_c(cc):
        t = cc.reshape(S5_NCH, gl, S5_H, S5_P).transpose(0, 1, 3, 2)
        t = t[:, :, :, None, :] * eye[None, :, None, :, None]
        return t.reshape(S5_NCH, S5_CS, S5_CU).astype(BF16)

    pr, pi = [lbr.reshape(-1)], [lbi.reshape(-1)]
    for _ in range(SUBLANES - 1):
        qr = pr[-1] * pr[0] - pi[-1] * pi[0]
        qi = pr[-1] * pi[0] + pi[-1] * pr[0]
        pr.append(qr)
        pi.append(qi)
    row = jnp.arange(SUBLANES)[:, None]
    tabs = []
    for dd in (1, 2, 4):
        keep = (row >= dd).astype(F32)
        tabs.append(keep * pr[dd - 1][None, :])
        tabs.append(keep * pi[dd - 1][None, :])
    tabs.append(jnp.stack(pr))
    tabs.append(jnp.stack(pi))
    return dict(
        wbr=pack_b(bbr), wbi=pack_b(bbi),
        wcr=pack_c(c_re.astype(F32)), wci=pack_c(-c_im.astype(F32)),
        tab=jnp.stack(tabs), d=d.astype(F32).reshape(1, D_S5),
        wglu=w_glu.astype(BF16), bglu=b_glu.astype(F32).reshape(1, D_S5),
    )


def _conv_piece(ext, x, halo, w_ref, b_ref, nseq, lseq):
    k = w_ref.shape[0]
    lo = SUBLANES - (k - 1)
    ext[:, lo:SUBLANES, :] = halo
    ext[:, SUBLANES:SUBLANES + lseq, :] = x.reshape(nseq, lseq, x.shape[-1])
    out = b_ref[...] + w_ref[0:1, :] * ext[:, lo:lo + lseq, :]
    for j in range(1, k):
        out = out + w_ref[j:j + 1, :] * ext[:, lo + j:lo + j + lseq, :]
    new = ext[:, lseq + lo:lseq + SUBLANES, :]
    return out.reshape(nseq * lseq, x.shape[-1]), new


def _ssd_kernel(*refs, nseq, lseq, prompt):
    if prompt:
        (z_ref, xs_ref, b_ref, c_ref, dt_ref, wx_ref, wb_ref, wc_ref, bx_ref, bb_ref, bc_ref,
         a_ref, dtb_ref, d_ref, nw_ref,
         y_ref, cnx_ref, cnb_ref, cnc_ref, hn_ref, extx, extb, extc, h_scr) = refs
        first = pl.program_id(2) == 0

        @pl.when(first)
        def _():
            h_scr[...] = jnp.zeros_like(h_scr)
            extx[...] = jnp.zeros_like(extx)
            extb[...] = jnp.zeros_like(extb)
            extc[...] = jnp.zeros_like(extc)

        lo = SUBLANES - (SSD_CONV - 1)
        halos = [e[:, lseq + lo:lseq + SUBLANES, :] for e in (extx, extb, extc)]
    else:
        (z_ref, xs_ref, b_ref, c_ref, dt_ref, wx_ref, wb_ref, wc_ref, bx_ref, bb_ref, bc_ref,
         a_ref, dtb_ref, d_ref, nw_ref, sx_ref, sb_ref, sc_ref, h0_ref,
         y_ref, cnx_ref, cnb_ref, cnc_ref, hn_ref, extx, extb, extc) = refs
        halos = [sx_ref[...], sb_ref[...], sc_ref[...]]

    r = nseq * lseq
    xs, newx = _conv_piece(extx, xs_ref[...], halos[0], wx_ref, bx_ref, nseq, lseq)
    bm, newb = _conv_piece(extb, b_ref[...], halos[1], wb_ref, bb_ref, nseq, lseq)
    cm, newc = _conv_piece(extc, c_ref[...], halos[2], wc_ref, bc_ref, nseq, lseq)
    cnx_ref[...] = newx
    cnb_ref[...] = newb
    cnc_ref[...] = newc
    xs = jax.nn.silu(xs)
    bmb = jax.nn.silu(bm).astype(BF16)
    cmb = jax.nn.silu(cm).astype(BF16)

    lane = lax.broadcasted_iota(jnp.int32, (r, SSD_GW), 1)
    head = lane >> int(math.log2(SSD_HEADDIM))
    dt4 = dt_ref[...]
    dtr = jnp.broadcast_to(dt4[:, 3:4], (r, SSD_GW))
    for k in (2, 1, 0):
        dtr = jnp.where(head == k, jnp.broadcast_to(dt4[:, k:k + 1], (r, SSD_GW)), dtr)
    dt = jax.nn.softplus(dtr + dtb_ref[...])
    a = dt * a_ref[...]

    li = lax.broadcasted_iota(jnp.int32, (r, r), 0)
    si = lax.broadcasted_iota(jnp.int32, (r, r), 1)
    sh = int(math.log2(lseq))
    same = (li >> sh) == (si >> sh)
    causal = same & (si <= li)
    acs = jnp.dot(causal.astype(F32), a, preferred_element_type=F32, precision=lax.Precision.HIGHEST)
    atot = jnp.dot(same.astype(F32), a, preferred_element_type=F32, precision=lax.Precision.HIGHEST)
    acs_t = acs.T

    xd = xs * dt
    xdb = xd.astype(BF16)
    cb = _dot_nt(cmb, bmb)
    y = jnp.zeros((r, SSD_GW), F32)
    for k in range(SSD_GW // SSD_HEADDIM):
        col = acs[:, k * SSD_HEADDIM:k * SSD_HEADDIM + 1]
        row = acs_t[k * SSD_HEADDIM:k * SSD_HEADDIM + 1, :]
        lmat = jnp.exp(jnp.where(causal, col - row, -jnp.inf))
        yk = _dot((cb * lmat).astype(BF16), xdb)
        y = jnp.where(head == k, yk, y)

    eacs = jnp.exp(acs)
    xdd_t = (xd * jnp.exp(atot - acs)).T.astype(BF16)
    col_seq = lax.broadcasted_iota(jnp.int32, (SSD_GW, r), 1) >> sh
    yoffs = []
    for s in range(nseq):
        rows = slice(s * lseq, (s + 1) * lseq)
        hprev = h_scr[...] if prompt else h0_ref[s].reshape(SSD_GW, SSD_STATE)
        yoffs.append(_dot_nt(cmb[rows], hprev.astype(BF16)) * eacs[rows])
        xs_t = xdd_t if nseq == 1 else jnp.where(col_seq == s, xdd_t, jnp.zeros_like(xdd_t))
        cd = jnp.exp(acs_t[:, (s + 1) * lseq - 1:(s + 1) * lseq])
        hnew = cd * hprev + _dot(xs_t, bmb)
        if prompt:
            h_scr[...] = hnew
            hn_ref[...] = hnew.reshape(hn_ref.shape)
        else:
            hn_ref[s] = hnew.reshape(hn_ref.shape[1:])
    yoff = yoffs[0] if nseq == 1 else jnp.concatenate(yoffs, axis=0)

    y = y + yoff + d_ref[...] * xs
    y = y * jax.nn.silu(z_ref[...])
    y_ref[...] = (_rms(y) * nw_ref[...]).astype(BF16)


def _ssd(proj, dtg, st_conv, st_h, w, *, batch, seqlen):
    m = proj.shape[0]
    prompt = st_conv is None
    r = SSD_R
    gw, ns = SSD_GW, SSD_STATE
    zc, xc, bc, cc = D_S5 // gw, (D_S5 + D_SSD) // gw, (D_S5 + 2 * D_SSD) // ns, (D_S5 + 2 * D_SSD) // ns + SSD_GROUPS
    hpg = SSD_HEADS // SSD_GROUPS
    if prompt:
        nseq, lseq = 1, r
        nc = seqlen // r
        grid = (batch, SSD_GROUPS, nc)
        row = lambda b, g, c: b * nc + c
        grp = lambda b, g, c: g
        seq = lambda b, g, c: b
        sem = ("arbitrary", "arbitrary", "arbitrary")
    else:
        lseq = seqlen
        nseq = r // lseq
        grid = (m // r, SSD_GROUPS)
        row = lambda i, g: i
        grp = lambda i, g: g
        seq = lambda i, g: i
        sem = ("arbitrary", "arbitrary")

    def rc(width, coff):
        return pl.BlockSpec((r, width), lambda *a: (row(*a), coff + grp(*a)))

    def pc(rows_, width, coff):
        return pl.BlockSpec((rows_, width), lambda *a: (0, coff + grp(*a)))

    in_specs = [
        rc(gw, zc), rc(gw, xc), rc(ns, bc), rc(ns, cc),
        pl.BlockSpec((None, r, hpg), lambda *a: (grp(*a), row(*a), 0)),
        pc(SSD_CONV, gw, 0), pc(SSD_CONV, ns, D_SSD // ns), pc(SSD_CONV, ns, D_SSD // ns + SSD_GROUPS),
        pc(1, gw, 0), pc(1, ns, D_SSD // ns), pc(1, ns, D_SSD // ns + SSD_GROUPS),
        pc(1, gw, 0), pc(1, gw, 0), pc(1, gw, 0), pc(1, gw, 0),
    ]
    args = [proj, proj, proj, proj, dtg, w["conv_w"], w["conv_w"], w["conv_w"], w["conv_b"], w["conv_b"], w["conv_b"],
            w["a"], w["dtb"], w["d"], w["nw"]]
    km1 = SSD_CONV - 1
    if not prompt:
        in_specs += [
            pl.BlockSpec((nseq, km1, gw), lambda *a: (seq(*a), 0, grp(*a))),
            pl.BlockSpec((nseq, km1, ns), lambda *a: (seq(*a), 0, D_SSD // ns + grp(*a))),
            pl.BlockSpec((nseq, km1, ns), lambda *a: (seq(*a), 0, D_SSD // ns + SSD_GROUPS + grp(*a))),
            pl.BlockSpec((nseq, hpg, SSD_HEADDIM, ns), lambda *a: (seq(*a), grp(*a), 0, 0)),
        ]
        args += [st_conv, st_conv, st_conv, st_h]
    out_specs = [
        pl.BlockSpec((r, gw), lambda *a: (row(*a), grp(*a))),
        pl.BlockSpec((nseq, km1, gw), lambda *a: (seq(*a), 0, grp(*a))),
        pl.BlockSpec((nseq, km1, ns), lambda *a: (seq(*a), 0, grp(*a))),
        pl.BlockSpec((nseq, km1, ns), lambda *a: (seq(*a), 0, grp(*a))),
        pl.BlockSpec((nseq, hpg, SSD_HEADDIM, ns), lambda *a: (seq(*a), grp(*a), 0, 0)),
    ]
    out_shape = [
        jax.ShapeDtypeStruct((m, D_SSD), BF16),
        jax.ShapeDtypeStruct((batch, km1, D_SSD), F32),
        jax.ShapeDtypeStruct((batch, km1, SSD_GROUPS * ns), F32),
        jax.ShapeDtypeStruct((batch, km1, SSD_GROUPS * ns), F32),
        jax.ShapeDtypeStruct((batch, SSD_HEADS, SSD_HEADDIM, ns), F32),
    ]
    scratch = [pltpu.VMEM((nseq, lseq + SUBLANES, gw), F32), pltpu.VMEM((nseq, lseq + SUBLANES, ns), F32),
               pltpu.VMEM((nseq, lseq + SUBLANES, ns), F32)]
    if prompt:
        scratch.append(pltpu.VMEM((gw, ns), F32))
    return pl.pallas_call(
        functools.partial(_ssd_kernel, nseq=nseq, lseq=lseq, prompt=prompt),
        grid=grid, in_specs=in_specs, out_specs=out_specs, out_shape=out_shape,
        scratch_shapes=scratch, compiler_params=_cparams(sem),
        name="ssd_prompt" if prompt else "ssd_sample",
    )(*args)


def _outproj_kernel(ya_ref, yb_ref, wa_ref, wb_ref, x_ref, gpost_ref, gpre_ref, x1_ref, hf_ref, mix_scr, *, tn):
    j = pl.program_id(1)
    col = pl.multiple_of(j * tn, tn)
    mix_scr[:, pl.ds(col, tn)] = _dot(ya_ref[...], wa_ref[...]) + _dot(yb_ref[...], wb_ref[...])

    @pl.when(j == pl.num_programs(1) - 1)
    def _():
        x1 = x_ref[...] + _rms(mix_scr[...]) * gpost_ref[...]
        x1_ref[...] = x1
        hf_ref[...] = (_rms(x1) * gpre_ref[...]).astype(BF16)


def _outproj(ya, yb, w_out_bf, x2d, gpost, gpre, tm=256, tn=1024):
    m = x2d.shape[0]
    return pl.pallas_call(
        functools.partial(_outproj_kernel, tn=tn),
        grid=(m // tm, D_MODEL // tn),
        in_specs=[
            pl.BlockSpec((tm, D_S5), lambda i, j: (i, 0)),
            pl.BlockSpec((tm, D_SSD), lambda i, j: (i, 0)),
            pl.BlockSpec((D_S5, tn), lambda i, j: (0, j)),
            pl.BlockSpec((D_SSD, tn), lambda i, j: (1, j)),
            pl.BlockSpec((tm, D_MODEL), lambda i, j: (i, 0)),
            pl.BlockSpec((1, D_MODEL), lambda i, j: (0, 0)),
            pl.BlockSpec((1, D_MODEL), lambda i, j: (0, 0)),
        ],
        out_specs=[pl.BlockSpec((tm, D_MODEL), lambda i, j: (i, 0)),
                   pl.BlockSpec((tm, D_MODEL), lambda i, j: (i, 0))],
        out_shape=[jax.ShapeDtypeStruct((m, D_MODEL), F32), jax.ShapeDtypeStruct((m, D_MODEL), BF16)],
        scratch_shapes=[pltpu.VMEM((tm, D_MODEL), F32)],
        compiler_params=_cparams(("arbitrary", "arbitrary")),
        name="outproj",
    )(ya, yb, w_out_bf, w_out_bf, x2d, gpost, gpre)


def _ffn_up_kernel(*refs, nseq, lseq, tiles_per_seq, tn):
    if nseq == 1:
        (hf_ref, wg_ref, wv_ref, cwg_ref, cwv_ref, cbg_ref, cbv_ref,
         act_ref, ng_ref, nv_ref, extg, extv, halo_g, halo_v) = refs
        j = pl.program_id(1)
        col = pl.multiple_of(j * tn, tn)
        lo = SUBLANES - (FFN_CONV - 1)
        fresh = (pl.program_id(0) % tiles_per_seq) == 0
        hg = jnp.where(fresh, 0.0, halo_g[lo:SUBLANES, pl.ds(col, tn)])[None]
        hv = jnp.where(fresh, 0.0, halo_v[lo:SUBLANES, pl.ds(col, tn)])[None]
    else:
        (hf_ref, wg_ref, wv_ref, cwg_ref, cwv_ref, cbg_ref, cbv_ref, sg_ref, sv_ref,
         act_ref, ng_ref, nv_ref, extg, extv) = refs
        hg = sg_ref[...]
        hv = sv_ref[...]
    hf = hf_ref[...]
    gate, newg = _conv_piece(extg, _dot(hf, wg_ref[...]), hg, cwg_ref, cbg_ref, nseq, lseq)
    val, newv = _conv_piece(extv, _dot(hf, wv_ref[...]), hv, cwv_ref, cbv_ref, nseq, lseq)
    ng_ref[...] = newg
    nv_ref[...] = newv
    if nseq == 1:
        halo_g[lo:SUBLANES, pl.ds(col, tn)] = newg[0]
        halo_v[lo:SUBLANES, pl.ds(col, tn)] = newv[0]
    act_ref[...] = (jax.nn.gelu(gate, approximate=True) * val).astype(BF16)


def _ffn_up(hf, w_up_bf, conv_w, conv_b, st, *, batch, seqlen, tm=1024, tn=256):
    m = hf.shape[0]
    nj = D_FF // tn
    km1 = FFN_CONV - 1
    prompt = st is None
    if prompt:
        nseq, lseq = 1, tm
        tps = seqlen // tm
        seq_blk = lambda i: i // tps
    else:
        lseq = seqlen
        nseq = tm // lseq
        tps = 1
        seq_blk = lambda i: i
    in_specs = [
        pl.BlockSpec((tm, D_MODEL), lambda i, j: (i, 0)),
        pl.BlockSpec((D_MODEL, tn), lambda i, j: (0, j)),
        pl.BlockSpec((D_MODEL, tn), lambda i, j: (0, nj + j)),
        pl.BlockSpec((FFN_CONV, tn), lambda i, j: (0, j)),
        pl.BlockSpec((FFN_CONV, tn), lambda i, j: (0, nj + j)),
        pl.BlockSpec((1, tn), lambda i, j: (0, j)),
        pl.BlockSpec((1, tn), lambda i, j: (0, nj + j)),
    ]
    args = [hf, w_up_bf, w_up_bf, conv_w, conv_w, conv_b, conv_b]
    scratch = [pltpu.VMEM((nseq, lseq + SUBLANES, tn), F32), pltpu.VMEM((nseq, lseq + SUBLANES, tn), F32)]
    if prompt:
        scratch += [pltpu.VMEM((SUBLANES, D_FF), F32), pltpu.VMEM((SUBLANES, D_FF), F32)]
    else:
        in_specs += [pl.BlockSpec((nseq, km1, tn), lambda i, j: (seq_blk(i), 0, j)),
                     pl.BlockSpec((nseq, km1, tn), lambda i, j: (seq_blk(i), 0, nj + j))]
        args += [st, st]
    return pl.pallas_call(
        functools.partial(_ffn_up_kernel, nseq=nseq, lseq=lseq, tiles_per_seq=tps, tn=tn),
        grid=(m // tm, nj),
        in_specs=in_specs,
        out_specs=[pl.BlockSpec((tm, tn), lambda i, j: (i, j)),
                   pl.BlockSpec((nseq, km1, tn), lambda i, j: (seq_blk(i), 0, j)),
                   pl.BlockSpec((nseq, km1, tn), lambda i, j: (seq_blk(i), 0, j))],
        out_shape=[jax.ShapeDtypeStruct((m, D_FF), BF16),
                   jax.ShapeDtypeStruct((batch, km1, D_FF), F32),
                   jax.ShapeDtypeStruct((batch, km1, D_FF), F32)],
        scratch_shapes=scratch,
        compiler_params=_cparams(("arbitrary", "arbitrary")),
        name="ffn_up_prompt" if prompt else "ffn_up_sample",
    )(*args)


def _ffn_down_kernel(act_ref, w_ref, x1_ref, g_ref, o_ref, acc_scr):
    k = pl.program_id(1)

    @pl.when(k == 0)
    def _():
        acc_scr[...] = jnp.zeros_like(acc_scr)

    acc_scr[...] += _dot(act_ref[...], w_ref[...])

    @pl.when(k == pl.num_programs(1) - 1)
    def _():
        o_ref[...] = x1_ref[...] + _rms(acc_scr[...]) * g_ref[...]


def _ffn_down(act, w_down_bf, x1, g, tm=512, tk=256):
    m = act.shape[0]
    return pl.pallas_call(
        _ffn_down_kernel,
        grid=(m // tm, D_FF // tk),
        in_specs=[
            pl.BlockSpec((tm, tk), lambda i, k: (i, k)),
            pl.BlockSpec((tk, D_MODEL), lambda i, k: (k, 0)),
            pl.BlockSpec((tm, D_MODEL), lambda i, k: (i, 0)),
            pl.BlockSpec((1, D_MODEL), lambda i, k: (0, 0)),
        ],
        out_specs=pl.BlockSpec((tm, D_MODEL), lambda i, k: (i, 0)),
        out_shape=jax.ShapeDtypeStruct((m, D_MODEL), F32),
        scratch_shapes=[pltpu.VMEM((tm, D_MODEL), F32)],
        compiler_params=_cparams(("arbitrary", "arbitrary")),
        name="ffn_down",
    )(act, w_down_bf, x1, g)


def _layer(x, states, w):
    b, l, _ = x.shape
    m = b * l
    x2d = x.reshape(m, D_MODEL)
    proj, dt_raw = _inproj(x2d, w["g_mix_pre"], w["w_in"], w["w_dt"])
    dtg = dt_raw.reshape(m, SSD_GROUPS, SSD_HEADS // SSD_GROUPS).transpose(1, 0, 2)
    nstate = S5_G * S5_P
    if states is None:
        ya, s5r, s5i = _s5(proj, None, None, w["s5"], batch=b, seqlen=l)
        yb, cnx, cnb, cnc, hn = _ssd(proj, dtg, None, None, w["ssd"], batch=b, seqlen=l)
    else:
        s5r0, s5i0, conv0, h0, fconv0 = states
        ya, s5r, s5i = _s5(proj, s5r0.reshape(b, nstate), s5i0.reshape(b, nstate), w["s5"], batch=b, seqlen=l)
        yb, cnx, cnb, cnc, hn = _ssd(proj, dtg, conv0, h0, w["ssd"], batch=b, seqlen=l)
    x1, hf = _outproj(ya, yb, w["w_out"], x2d, w["g_mix_post"], w["g_ffn_pre"])
    act, ng, nv = _ffn_up(hf, w["w_up"], w["ffn_conv_w"], w["ffn_conv_b"],
                          None if states is None else fconv0, batch=b, seqlen=l)
    y = _ffn_down(act, w["w_down"], x1, w["g_ffn_post"])
    return (y.reshape(b, l, D_MODEL),
            s5r.reshape(1, b, S5_G, S5_P), s5i.reshape(1, b, S5_G, S5_P),
            jnp.concatenate([cnx, cnb, cnc], axis=-1)[None], hn[None],
            jnp.concatenate([ng, nv], axis=-1)[None])


def kernel(x_prompt, x_sample, state_s5_re, state_s5_im, state_ssd_conv, state_ssd, state_ffn_conv,
           g_mix_pre, g_mix_post, g_ffn_pre, g_ffn_post, w_in,
           s5_a_re, s5_a_im, s5_log_dt, s5_b_re, s5_b_im, s5_c_re, s5_c_im, s5_d, s5_w_glu, s5_b_glu,
           ssd_conv_w, ssd_conv_b, ssd_dt_bias, ssd_a_log, ssd_d, ssd_norm_w,
           w_out, ffn_w_up, ffn_conv_w, ffn_conv_b, ffn_w_down):
    w_in_bf = w_in[0].astype(BF16)
    rep = lambda v: jnp.repeat(v.astype(F32), SSD_HEADDIM).reshape(1, D_SSD)
    w = dict(
        g_mix_pre=g_mix_pre[0].reshape(1, D_MODEL), g_mix_post=g_mix_post[0].reshape(1, D_MODEL),
        g_ffn_pre=g_ffn_pre[0].reshape(1, D_MODEL), g_ffn_post=g_ffn_post[0].reshape(1, D_MODEL),
        w_in=w_in_bf, w_dt=w_in_bf[:, D_PROJ:],
        s5=_s5_weights(s5_a_re[0], s5_a_im[0], s5_log_dt[0], s5_b_re[0], s5_b_im[0], s5_c_re[0], s5_c_im[0],
                       s5_d[0], s5_w_glu[0], s5_b_glu[0]),
        ssd=dict(conv_w=ssd_conv_w[0], conv_b=ssd_conv_b[0].reshape(1, D_XBC),
                 a=rep(-jnp.exp(ssd_a_log[0].astype(F32))), dtb=rep(ssd_dt_bias[0]), d=rep(ssd_d[0]),
                 nw=ssd_norm_w[0].reshape(1, D_SSD)),
        w_out=w_out[0].astype(BF16), w_up=ffn_w_up[0].astype(BF16),
        ffn_conv_w=ffn_conv_w[0], ffn_conv_b=ffn_conv_b[0].reshape(1, 2 * D_FF),
        w_down=ffn_w_down[0].astype(BF16),
    )
    outs_p = _layer(x_prompt, None, w)
    outs_s = _layer(x_sample, (state_s5_re[0], state_s5_im[0], state_ssd_conv[0], state_ssd[0], state_ffn_conv[0]), w)
    return (outs_p[0], outs_s[0]) + outs_p[1:] + outs_s[1:]
```

```python
import functools
import math

import jax
import jax.numpy as jnp
import numpy as np
from jax import lax
from jax.experimental import pallas as pl
from jax.experimental.pallas import tpu as pltpu

D_MODEL = 4096
D_S5 = 2048
S5_H = 16
S5_G = 128
S5_P = 64
D_SSD = 2048
SSD_HEADDIM = 64
SSD_HEADS = 32
SSD_GROUPS = 8
SSD_STATE = 128
SSD_CONV = 4
D_XBC = 4096
D_PROJ = D_S5 + D_SSD + D_XBC
D_FF = 11008
FFN_CONV = 3
EPS = 1e-6

F32 = jnp.float32
BF16 = jnp.bfloat16

SUBLANES = 8
VMEM_LIMIT = 56 * 1024 * 1024

S5_NCH = 8
S5_CU = D_S5 // S5_NCH
S5_CS = S5_G * S5_P // S5_NCH
S5_ROWS = 256
S5_TK = S5_ROWS // SUBLANES
SSD_GW = D_SSD // SSD_GROUPS
SSD_HPG = SSD_HEADS // SSD_GROUPS
SSD_R = 128


def _cparams(sem):
    return pltpu.CompilerParams(dimension_semantics=sem, vmem_limit_bytes=VMEM_LIMIT)


def _rms(x):
    return x * lax.rsqrt(jnp.mean(x * x, axis=-1, keepdims=True) + EPS)


def _dot(a, b):
    return jnp.dot(a, b, preferred_element_type=F32)


def _dot_nt(a, b):
    return lax.dot_general(a, b, (((1,), (1,)), ((), ())), preferred_element_type=F32)


def _const_spec(shape):
    nd = len(shape)
    return pl.BlockSpec(shape, lambda *_: (0,) * nd, pipeline_mode=pl.Buffered(1))


def _conv_windows(ext, w_ref, b_ref, lseq):
    k = w_ref.shape[0]
    lo = SUBLANES - (k - 1)
    out = b_ref[...] + w_ref[0:1, :] * ext[:, lo:lo + lseq, :]
    for j in range(1, k):
        out = out + w_ref[j:j + 1, :] * ext[:, lo + j:lo + j + lseq, :]
    return out.reshape(ext.shape[0] * lseq, ext.shape[2])


def _conv_fill(ext, x, halo, k, lseq):
    lo = SUBLANES - (k - 1)
    x3 = x.reshape(ext.shape[0], lseq, ext.shape[2])
    ext[:, lo:SUBLANES, :] = halo
    ext[:, SUBLANES:SUBLANES + lseq, :] = x3
    return x3[:, lseq - (k - 1):, :]


def _inproj_kernel(x_ref, g_ref, w_ref, wdt_ref, o_ref, dt_ref, h_scr):
    @pl.when(pl.program_id(1) == 0)
    def _():
        hb = (_rms(x_ref[...]) * g_ref[...]).astype(BF16)
        h_scr[...] = hb
        dt_ref[...] = _dot(hb, wdt_ref[...])

    o_ref[...] = _dot(h_scr[...], w_ref[...])


def _inproj(x2d, g, w_in_bf, wdt_bf, tm=512, tn=1024):
    m = x2d.shape[0]
    return pl.pallas_call(
        _inproj_kernel,
        grid=(m // tm, D_PROJ // tn),
        in_specs=[
            pl.BlockSpec((tm, D_MODEL), lambda i, j: (i, 0)),
            pl.BlockSpec((1, D_MODEL), lambda i, j: (0, 0)),
            pl.BlockSpec((D_MODEL, tn), lambda i, j: (0, j)),
            pl.BlockSpec((D_MODEL, SSD_HEADS), lambda i, j: (0, 0)),
        ],
        out_specs=[
            pl.BlockSpec((tm, tn), lambda i, j: (i, j)),
            pl.BlockSpec((tm, SSD_HEADS), lambda i, j: (i, 0)),
        ],
        out_shape=[jax.ShapeDtypeStruct((m, D_PROJ), F32), jax.ShapeDtypeStruct((m, SSD_HEADS), F32)],
        scratch_shapes=[pltpu.VMEM((tm, D_MODEL), BF16)],
        compiler_params=_cparams(("arbitrary", "arbitrary")),
        name="inproj",
    )(x2d, g, w_in_bf, wdt_bf)


def _s5_kernel(*refs, rows, tk, prompt):
    if prompt:
        (u_ref, perm_ref, permt_ref, wbr_ref, wbi_ref, wcr_ref, wci_ref, lam_ref, mu_ref, d_ref,
         wglu_ref, bglu_ref, y_ref, sro_ref, sio_ref, sr_scr, si_scr, g_scr, car_scr) = refs

        @pl.when(pl.program_id(1) == 0)
        def _():
            car_scr[...] = jnp.zeros_like(car_scr)
    else:
        (u_ref, s0r_ref, s0i_ref, perm_ref, permt_ref, wbr_ref, wbi_ref, wcr_ref, wci_ref, lam_ref, d_ref,
         wglu_ref, bglu_ref, y_ref, sro_ref, sio_ref, sr_scr, si_scr, g_scr) = refs
    ngrp = rows // (SUBLANES * tk)
    perm = perm_ref[...]

    for c in range(S5_NCH):
        lanes = slice(c * S5_CS, (c + 1) * S5_CS)
        ulanes = slice(c * S5_CU, (c + 1) * S5_CU)
        u = u_ref[:, ulanes]
        u1 = u.astype(BF16)
        r1 = u - u1.astype(F32)
        u2 = r1.astype(BF16)
        u3 = (r1 - u2.astype(F32)).astype(BF16)
        p1 = _dot(perm, u1)
        up = p1 + _dot(perm, u2) + _dot(perm, u3)
        ub = p1.astype(BF16)
        sr_scr[...] = _dot(ub, wbr_ref[c])
        si_scr[...] = _dot(ub, wbi_ref[c])

        def step_a(k, x, base=0, lanes=lanes):
            xr, xi = x
            lam_r = lam_ref[0, :, lanes]
            lam_i = lam_ref[1, :, lanes]
            r0 = pl.multiple_of(base + k * SUBLANES, SUBLANES)
            nr = lam_r * xr - lam_i * xi + sr_scr[pl.ds(r0, SUBLANES), :]
            ni = lam_r * xi + lam_i * xr + si_scr[pl.ds(r0, SUBLANES), :]
            sr_scr[pl.ds(r0, SUBLANES), :] = nr
            si_scr[pl.ds(r0, SUBLANES), :] = ni
            return nr, ni

        if prompt:
            zero = jnp.zeros((SUBLANES, S5_CS), F32)
            er, ei = lax.fori_loop(0, tk, step_a, (zero, zero), unroll=2)
            row = lax.broadcasted_iota(jnp.int32, (SUBLANES, S5_CS), 0)
            vr = jnp.where(row == 0, car_scr[0:1, lanes], pltpu.roll(er, 1, axis=0))
            vi = jnp.where(row == 0, car_scr[1:2, lanes], pltpu.roll(ei, 1, axis=0))
            for q, d in enumerate((1, 2, 4)):
                ar = mu_ref[2 * q, :, lanes]
                ai = mu_ref[2 * q + 1, :, lanes]
                rr = pltpu.roll(vr, d, axis=0)
                ri = pltpu.roll(vi, d, axis=0)
                vr, vi = vr + ar * rr - ai * ri, vi + ar * ri + ai * rr
            mur = mu_ref[6, :, lanes]
            mui = mu_ref[7, :, lanes]
            outr = (mur * vr - mui * vi + er)[SUBLANES - 1:SUBLANES, :]
            outi = (mur * vi + mui * vr + ei)[SUBLANES - 1:SUBLANES, :]
            car_scr[0:1, lanes] = outr
            car_scr[1:2, lanes] = outi
            sro_ref[:, lanes] = outr
            sio_ref[:, lanes] = outi

            def step_c(k, x, lanes=lanes):
                xr, xi = x
                lam_r = lam_ref[0, :, lanes]
                lam_i = lam_ref[1, :, lanes]
                r0 = pl.multiple_of(k * SUBLANES, SUBLANES)
                nr = lam_r * xr - lam_i * xi
                ni = lam_r * xi + lam_i * xr
                sr_scr[pl.ds(r0, SUBLANES), :] = sr_scr[pl.ds(r0, SUBLANES), :] + nr
                si_scr[pl.ds(r0, SUBLANES), :] = si_scr[pl.ds(r0, SUBLANES), :] + ni
                return nr, ni

            lax.fori_loop(0, tk, step_c, (vr, vi), unroll=2)
        else:
            for q in range(ngrp):
                srows = slice(q * SUBLANES, (q + 1) * SUBLANES)
                x0 = (s0r_ref[srows, lanes], s0i_ref[srows, lanes])
                fr, fi = lax.fori_loop(0, tk, functools.partial(step_a, base=q * SUBLANES * tk), x0, unroll=2)
                sro_ref[srows, lanes] = fr
                sio_ref[srows, lanes] = fi

        yc = (_dot(sr_scr[...].astype(BF16), wcr_ref[c]) + _dot(si_scr[...].astype(BF16), wci_ref[c])
              + d_ref[:, ulanes] * up)
        g_scr[:, ulanes] = jax.nn.gelu(yc)

    g = g_scr[...]
    gl = _dot(g.astype(BF16), wglu_ref[...]) + bglu_ref[...]
    yp = (g * jax.nn.sigmoid(gl)).astype(BF16)
    y_ref[...] = _dot(permt_ref[...], yp).astype(BF16)


def _s5_perm(rows, tk, prompt):
    p = np.zeros((rows, rows), np.float32)
    per = SUBLANES * tk
    for q in range(rows // per):
        for k in range(tk):
            for r in range(SUBLANES):
                p[q * per + k * SUBLANES + r, q * per + r * tk + k] = 1.0
    return jnp.asarray(p, BF16), jnp.asarray(p.T, BF16)


def _s5(proj, s0r, s0i, w, *, batch, seqlen):
    m = proj.shape[0]
    rows = S5_ROWS
    prompt = s0r is None
    nstate = S5_G * S5_P
    tk = S5_TK if prompt else seqlen
    perm, permt = _s5_perm(rows, tk, prompt)
    mat_specs = [_const_spec((rows, rows)), _const_spec((rows, rows)),
                 _const_spec((S5_NCH, S5_CU, S5_CS)), _const_spec((S5_NCH, S5_CU, S5_CS)),
                 _const_spec((S5_NCH, S5_CS, S5_CU)), _const_spec((S5_NCH, S5_CS, S5_CU)),
                 _const_spec((2, SUBLANES, nstate))]
    mat_args = (perm, permt, w["wbr"], w["wbi"], w["wcr"], w["wci"], w["lam"])
    tail_specs = [_const_spec((1, D_S5)), _const_spec((D_S5, D_S5)), _const_spec((1, D_S5))]
    tail_args = (w["d"], w["wglu"], w["bglu"])
    scratch = [pltpu.VMEM((rows, S5_CS), F32), pltpu.VMEM((rows, S5_CS), F32), pltpu.VMEM((rows, D_S5), F32)]
    if prompt:
        nc = seqlen // rows
        grid = (batch, nc)
        in_specs = ([pl.BlockSpec((rows, D_S5), lambda b, c: (b * nc + c, 0))] + mat_specs
                    + [_const_spec((8, SUBLANES, nstate))] + tail_specs)
        out_specs = [
            pl.BlockSpec((rows, D_S5), lambda b, c: (b * nc + c, 0)),
            pl.BlockSpec((None, 1, nstate), lambda b, c: (b, 0, 0)),
            pl.BlockSpec((None, 1, nstate), lambda b, c: (b, 0, 0)),
        ]
        out_shape = [jax.ShapeDtypeStruct((m, D_S5), BF16),
                     jax.ShapeDtypeStruct((batch, 1, nstate), F32), jax.ShapeDtypeStruct((batch, 1, nstate), F32)]
        args = (proj,) + mat_args + (w["mu"],) + tail_args
        scratch = scratch + [pltpu.VMEM((SUBLANES, nstate), F32)]
        sem = ("arbitrary", "arbitrary")
    else:
        nseq = rows // seqlen
        grid = (m // rows,)
        in_specs = ([pl.BlockSpec((rows, D_S5), lambda i: (i, 0)),
                     pl.BlockSpec((nseq, nstate), lambda i: (i, 0)),
                     pl.BlockSpec((nseq, nstate), lambda i: (i, 0))] + mat_specs + tail_specs)
        out_specs = [pl.BlockSpec((rows, D_S5), lambda i: (i, 0)),
                     pl.BlockSpec((nseq, nstate), lambda i: (i, 0)),
                     pl.BlockSpec((nseq, nstate), lambda i: (i, 0))]
        out_shape = [jax.ShapeDtypeStruct((m, D_S5), BF16),
                     jax.ShapeDtypeStruct((batch, nstate), F32), jax.ShapeDtypeStruct((batch, nstate), F32)]
        args = (proj, s0r, s0i) + mat_args + tail_args
        sem = ("arbitrary",)
    return pl.pallas_call(
        functools.partial(_s5_kernel, rows=rows, tk=tk, prompt=prompt),
        grid=grid, in_specs=in_specs, out_specs=out_specs, out_shape=out_shape,
        scratch_shapes=scratch, compiler_params=_cparams(sem),
        name="s5_prompt" if prompt else "s5_sample",
    )(*args)


def _s5_weights(a_re, a_im, log_dt, b_re, b_im, c_re, c_im, d, w_glu, b_glu):
    ar = a_re.astype(F32)
    ai = a_im.astype(F32)
    dt = jnp.exp(log_dt.astype(F32))[:, None]
    mag = jnp.exp(ar * dt)
    lbr = mag * jnp.cos(ai * dt)
    lbi = mag * jnp.sin(ai * dt)
    den = ar * ar + ai * ai
    nr = lbr - 1.0
    fr = (nr * ar + lbi * ai) / den
    fi = (lbi * ar - nr * ai) / den
    b_re = b_re.astype(F32)
    b_im = b_im.astype(F32)
    bbr = fr[..., None] * b_re - fi[..., None] * b_im
    bbi = fr[..., None] * b_im + fi[..., None] * b_re
    gl = S5_G // S5_NCH
    eye = jnp.eye(gl, dtype=F32)

    def pack_b(bb):
        t = bb.reshape(S5_NCH, gl, S5_P, S5_H).transpose(0, 1, 3, 2)
        t = t[:, :, :, None, :] * eye[None, :, None, :, None]
        return t.reshape(S5_NCH, S5_CU, S5_CS).astype(BF16)

    def pack_c(cc):
        t = cc.reshape(S5_NCH, gl, S5_H, S5_P).transpose(0, 1, 3, 2)
        t = t[:, :, :, None, :] * eye[None, :, None, :, None]
        return t.reshape(S5_NCH, S5_CS, S5_CU).astype(BF16)

    def cmul(xr, xi, yr, yi):
        return xr * yr - xi * yi, xr * yi + xi * yr

    pr, pi = [lbr.reshape(-1)], [lbi.reshape(-1)]
    for _ in range(S5_TK - 1):
        qr, qi = cmul(pr[-1], pi[-1], pr[0], pi[0])
        pr.append(qr)
        pi.append(qi)
    m1 = (pr[-1], pi[-1])
    m2 = cmul(*m1, *m1)
    m4 = cmul(*m2, *m2)
    row = jnp.arange(SUBLANES)[:, None]
    mu = []
    for dd, (mr, mi) in zip((1, 2, 4), (m1, m2, m4)):
        keep = (row >= dd).astype(F32)
        mu.append(keep * mr[None, :])
        mu.append(keep * mi[None, :])
    ones = jnp.ones((SUBLANES, 1), F32)
    mu.append(ones * m1[0][None, :])
    mu.append(ones * m1[1][None, :])
    return dict(
        wbr=pack_b(bbr), wbi=pack_b(bbi),
        wcr=pack_c(c_re.astype(F32)), wci=pack_c(-c_im.astype(F32)),
        lam=jnp.stack([ones * pr[0][None, :], ones * pi[0][None, :]]), mu=jnp.stack(mu),
        d=d.astype(F32).reshape(1, D_S5),
        wglu=w_glu.astype(BF16), bglu=b_glu.astype(F32).reshape(1, D_S5),
    )


def _ssd_kernel(*refs, nseq, lseq, gper, prompt):
    if prompt:
        (z_ref, xs_ref, b_ref, c_ref, dt_ref, wx_ref, wb_ref, wc_ref, bx_ref, bb_ref, bc_ref,
         a_ref, dtb_ref, d_ref, nw_ref,
         y_ref, cnx_ref, cnb_ref, cnc_ref, hn_ref, extx, extb, extc, h_scr) = refs

        @pl.when(pl.program_id(2) == 0)
        def _():
            h_scr[...] = jnp.zeros_like(h_scr)
            extx[...] = jnp.zeros_like(extx)
            extb[...] = jnp.zeros_like(extb)
            extc[...] = jnp.zeros_like(extc)

        lo = SUBLANES - (SSD_CONV - 1)
        halos = [e[:, lseq + lo:lseq + SUBLANES, :] for e in (extx, extb, extc)]
    else:
        (z_ref, xs_ref, b_ref, c_ref, dt_ref, wx_ref, wb_ref, wc_ref, bx_ref, bb_ref, bc_ref,
         a_ref, dtb_ref, d_ref, nw_ref, sx_ref, sb_ref, sc_ref, h0_ref,
         y_ref, cnx_ref, cnb_ref, cnc_ref, hn_ref, extx, extb, extc) = refs
        halos = [sx_ref[...], sb_ref[...], sc_ref[...]]

    r = nseq * lseq
    cnx_ref[...] = _conv_fill(extx, xs_ref[...], halos[0], SSD_CONV, lseq)
    cnb_ref[...] = _conv_fill(extb, b_ref[...], halos[1], SSD_CONV, lseq)
    cnc_ref[...] = _conv_fill(extc, c_ref[...], halos[2], SSD_CONV, lseq)
    xs_all = jax.nn.silu(_conv_windows(extx, wx_ref, bx_ref, lseq))
    bm_all = jax.nn.silu(_conv_windows(extb, wb_ref, bb_ref, lseq)).astype(BF16)
    cm_all = jax.nn.silu(_conv_windows(extc, wc_ref, bc_ref, lseq)).astype(BF16)

    lane = lax.broadcasted_iota(jnp.int32, (r, SSD_GW), 1)
    head = lane >> int(math.log2(SSD_HEADDIM))
    li = lax.broadcasted_iota(jnp.int32, (r, r), 0)
    si = lax.broadcasted_iota(jnp.int32, (r, r), 1)
    sh = int(math.log2(lseq))
    same = (li >> sh) == (si >> sh)
    causal = same & (si <= li)
    causal_f = causal.astype(F32)
    same_f = same.astype(F32)
    col_seq = lax.broadcasted_iota(jnp.int32, (SSD_GW, r), 1) >> sh

    for g in range(gper):
        gl = slice(g * SSD_GW, (g + 1) * SSD_GW)
        sl = slice(g * SSD_STATE, (g + 1) * SSD_STATE)
        xs = xs_all[:, gl]
        bmb = bm_all[:, sl]
        cmb = cm_all[:, sl]

        dt4 = dt_ref[:, g * SSD_HPG:(g + 1) * SSD_HPG]
        dtr = jnp.broadcast_to(dt4[:, SSD_HPG - 1:SSD_HPG], (r, SSD_GW))
        for k in range(SSD_HPG - 2, -1, -1):
            dtr = jnp.where(head == k, jnp.broadcast_to(dt4[:, k:k + 1], (r, SSD_GW)), dtr)
        dt = jax.nn.softplus(dtr + dtb_ref[:, gl])
        a = dt * a_ref[:, gl]

        acs = jnp.dot(causal_f, a, preferred_element_type=F32, precision=lax.Precision.HIGHEST)
        atot = jnp.dot(same_f, a, preferred_element_type=F32, precision=lax.Precision.HIGHEST)
        acs_t = acs.T

        xd = xs * dt
        xdb = xd.astype(BF16)
        cb = _dot_nt(cmb, bmb)
        y = jnp.zeros((r, SSD_GW), F32)
        for k in range(SSD_HPG):
            col = acs[:, k * SSD_HEADDIM:k * SSD_HEADDIM + 1]
            row = acs_t[k * SSD_HEADDIM:k * SSD_HEADDIM + 1, :]
            lmat = jnp.exp(jnp.where(causal, col - row, -jnp.inf))
            yk = _dot((cb * lmat).astype(BF16), xdb)
            y = jnp.where(head == k, yk, y)

        eacs = jnp.exp(acs)
        xdd_t = (xd * jnp.exp(atot - acs)).T.astype(BF16)
        hrows = slice(g * SSD_GW, (g + 1) * SSD_GW)
        hh = slice(g * SSD_HPG, (g + 1) * SSD_HPG)
        yoffs = []
        for s in range(nseq):
            rows = slice(s * lseq, (s + 1) * lseq)
            hprev = h_scr[hrows, :] if prompt else h0_ref[s, hh].reshape(SSD_GW, SSD_STATE)
            yoffs.append(_dot_nt(cmb[rows], hprev.astype(BF16)) * eacs[rows])
            xs_t = xdd_t if nseq == 1 else jnp.where(col_seq == s, xdd_t, jnp.zeros_like(xdd_t))
            cd = jnp.exp(acs_t[:, (s + 1) * lseq - 1:(s + 1) * lseq])
            hnew = cd * hprev + _dot(xs_t, bmb)
            if prompt:
                h_scr[hrows, :] = hnew
            hn_ref[s, hh] = hnew.reshape(SSD_HPG, SSD_HEADDIM, SSD_STATE)
        yoff = yoffs[0] if nseq == 1 else jnp.concatenate(yoffs, axis=0)

        y = y + yoff + d_ref[:, gl] * xs
        y = y * jax.nn.silu(z_ref[:, gl])
        y_ref[:, gl] = (_rms(y) * nw_ref[:, gl]).astype(BF16)


def _ssd(proj, dtg, st_conv, st_h, w, *, batch, seqlen, gper):
    m = proj.shape[0]
    prompt = st_conv is None
    r = SSD_R
    gw, ns = SSD_GW * gper, SSD_STATE * gper
    ngs = SSD_GROUPS // gper
    zc, xc, bc = D_S5 // gw, (D_S5 + D_SSD) // gw, (D_S5 + 2 * D_SSD) // ns
    cc = bc + ngs
    if prompt:
        nseq, lseq = 1, r
        nc = seqlen // r
        grid = (batch, ngs, nc)
        row = lambda b, g, c: b * nc + c
        grp = lambda b, g, c: g
        seq = lambda b, g, c: b
        sem = ("arbitrary", "arbitrary", "arbitrary")
    else:
        lseq = seqlen
        nseq = r // lseq
        grid = (m // r, ngs)
        row = lambda i, g: i
        grp = lambda i, g: g
        seq = lambda i, g: i
        sem = ("arbitrary", "arbitrary")

    def rc(width, coff):
        return pl.BlockSpec((r, width), lambda *a: (row(*a), coff + grp(*a)))

    def pc(rows_, width, coff):
        return pl.BlockSpec((rows_, width), lambda *a: (0, coff + grp(*a)))

    cb_, cc_ = D_SSD // ns, D_SSD // ns + ngs
    in_specs = [
        rc(gw, zc), rc(gw, xc), rc(ns, bc), rc(ns, cc),
        pl.BlockSpec((None, r, SSD_HPG * gper), lambda *a: (grp(*a), row(*a), 0)),
        pc(SSD_CONV, gw, 0), pc(SSD_CONV, ns, cb_), pc(SSD_CONV, ns, cc_),
        pc(1, gw, 0), pc(1, ns, cb_), pc(1, ns, cc_),
        pc(1, gw, 0), pc(1, gw, 0), pc(1, gw, 0), pc(1, gw, 0),
    ]
    args = [proj, proj, proj, proj, dtg, w["conv_w"], w["conv_w"], w["conv_w"], w["conv_b"], w["conv_b"], w["conv_b"],
            w["a"], w["dtb"], w["d"], w["nw"]]
    km1 = SSD_CONV - 1
    hb = SSD_HPG * gper
    if not prompt:
        in_specs += [
            pl.BlockSpec((nseq, km1, gw), lambda *a: (seq(*a), 0, grp(*a))),
            pl.BlockSpec((nseq, km1, ns), lambda *a: (seq(*a), 0, cb_ + grp(*a))),
            pl.BlockSpec((nseq, km1, ns), lambda *a: (seq(*a), 0, cc_ + grp(*a))),
            pl.BlockSpec((nseq, hb, SSD_HEADDIM, SSD_STATE), lambda *a: (seq(*a), grp(*a), 0, 0)),
        ]
        args += [st_conv, st_conv, st_conv, st_h]
    out_specs = [
        pl.BlockSpec((r, gw), lambda *a: (row(*a), grp(*a))),
        pl.BlockSpec((nseq, km1, gw), lambda *a: (seq(*a), 0, grp(*a))),
        pl.BlockSpec((nseq, km1, ns), lambda *a: (seq(*a), 0, grp(*a))),
        pl.BlockSpec((nseq, km1, ns), lambda *a: (seq(*a), 0, grp(*a))),
        pl.BlockSpec((nseq, hb, SSD_HEADDIM, SSD_STATE), lambda *a: (seq(*a), grp(*a), 0, 0)),
    ]
    out_shape = [
        jax.ShapeDtypeStruct((m, D_SSD), BF16),
        jax.ShapeDtypeStruct((batch, km1, D_SSD), F32),
        jax.ShapeDtypeStruct((batch, km1, SSD_GROUPS * SSD_STATE), F32),
        jax.ShapeDtypeStruct((batch, km1, SSD_GROUPS * SSD_STATE), F32),
        jax.ShapeDtypeStruct((batch, SSD_HEADS, SSD_HEADDIM, SSD_STATE), F32),
    ]
    scratch = [pltpu.VMEM((nseq, lseq + SUBLANES, gw), F32), pltpu.VMEM((nseq, lseq + SUBLANES, ns), F32),
               pltpu.VMEM((nseq, lseq + SUBLANES, ns), F32)]
    if prompt:
        scratch.append(pltpu.VMEM((gw, SSD_STATE), F32))
    return pl.pallas_call(
        functools.partial(_ssd_kernel, nseq=nseq, lseq=lseq, gper=gper, prompt=prompt),
        grid=grid, in_specs=in_specs, out_specs=out_specs, out_shape=out_shape,
        scratch_shapes=scratch, compiler_params=_cparams(sem),
        name="ssd_prompt" if prompt else "ssd_sample",
    )(*args)


def _outproj_kernel(ya_ref, yb_ref, wa_ref, wb_ref, x_ref, gpost_ref, gpre_ref, x1_ref, hf_ref, *, tn):
    j = pl.program_id(1)
    col = pl.multiple_of(j * tn, tn)
    x1_ref[:, pl.ds(col, tn)] = _dot(ya_ref[...], wa_ref[...]) + _dot(yb_ref[...], wb_ref[...])

    @pl.when(j == pl.num_programs(1) - 1)
    def _():
        x1 = x_ref[...] + _rms(x1_ref[...]) * gpost_ref[...]
        x1_ref[...] = x1
        hf_ref[...] = (_rms(x1) * gpre_ref[...]).astype(BF16)


def _outproj(ya, yb, w_out_bf, x2d, gpost, gpre, tm=512, tn=512):
    m = x2d.shape[0]
    return pl.pallas_call(
        functools.partial(_outproj_kernel, tn=tn),
        grid=(m // tm, D_MODEL // tn),
        in_specs=[
            pl.BlockSpec((tm, D_S5), lambda i, j: (i, 0)),
            pl.BlockSpec((tm, D_SSD), lambda i, j: (i, 0)),
            pl.BlockSpec((D_S5, tn), lambda i, j: (0, j)),
            pl.BlockSpec((D_SSD, tn), lambda i, j: (1, j)),
            pl.BlockSpec((tm, D_MODEL), lambda i, j: (i, 0), pipeline_mode=pl.Buffered(1)),
            pl.BlockSpec((1, D_MODEL), lambda i, j: (0, 0)),
            pl.BlockSpec((1, D_MODEL), lambda i, j: (0, 0)),
        ],
        out_specs=[pl.BlockSpec((tm, D_MODEL), lambda i, j: (i, 0)),
                   pl.BlockSpec((tm, D_MODEL), lambda i, j: (i, 0))],
        out_shape=[jax.ShapeDtypeStruct((m, D_MODEL), F32), jax.ShapeDtypeStruct((m, D_MODEL), BF16)],
        compiler_params=_cparams(("arbitrary", "arbitrary")),
        name="outproj",
    )(ya, yb, w_out_bf, w_out_bf, x2d, gpost, gpre)


def _ffn_kernel(*refs, nseq, lseq, tiles_per_seq, tn, nj):
    if nseq == 1:
        (hf_ref, wg_ref, wv_ref, cwg_ref, cwv_ref, cbg_ref, cbv_ref, wd_ref, x1_ref, gp_ref,
         o_ref, ng_ref, nv_ref, extg, extv, stg_g, stg_v, halo_g, halo_v) = refs
    else:
        (hf_ref, wg_ref, wv_ref, cwg_ref, cwv_ref, cbg_ref, cbv_ref, wd_ref, x1_ref, gp_ref, sg_ref, sv_ref,
         o_ref, ng_ref, nv_ref, extg, extv, stg_g, stg_v) = refs
    i = pl.program_id(0)
    j = pl.program_id(1)

    @pl.when((i == 0) & (j == 0))
    def _():
        extg[...] = jnp.zeros_like(extg)
        extv[...] = jnp.zeros_like(extv)

    @pl.when(j == 0)
    def _():
        o_ref[...] = jnp.zeros_like(o_ref)

    hf = hf_ref[...]
    stg_g[...] = _dot(hf, wg_ref[...])
    stg_v[...] = _dot(hf, wv_ref[...])

    gate = _conv_windows(extg, cwg_ref, cbg_ref, lseq)
    val = _conv_windows(extv, cwv_ref, cbv_ref, lseq)
    act = jax.nn.gelu(gate, approximate=True) * val
    act = jnp.where(j > 0, act, 0.0).astype(BF16)
    o_ref[...] += _dot(act, wd_ref[...])

    ug = stg_g[...]
    uv = stg_v[...]
    if nseq == 1:
        lo = SUBLANES - (FFN_CONV - 1)
        col = pl.multiple_of(jnp.minimum(j, nj - 1) * tn, tn)
        fresh = lax.rem(i, tiles_per_seq) == 0
        hg = jnp.where(fresh, 0.0, halo_g[lo:SUBLANES, pl.ds(col, tn)])[None]
        hv = jnp.where(fresh, 0.0, halo_v[lo:SUBLANES, pl.ds(col, tn)])[None]
    else:
        hg = sg_ref[...]
        hv = sv_ref[...]
    newg = _conv_fill(extg, ug, hg, FFN_CONV, lseq)
    newv = _conv_fill(extv, uv, hv, FFN_CONV, lseq)
    ng_ref[...] = newg
    nv_ref[...] = newv
    if nseq == 1:
        halo_g[lo:SUBLANES, pl.ds(col, tn)] = newg[0]
        halo_v[lo:SUBLANES, pl.ds(col, tn)] = newv[0]

    @pl.when(j == nj)
    def _():
        o_ref[...] = x1_ref[...] + _rms(o_ref[...]) * gp_ref[...]


def _ffn(hf, w_up_bf, conv_w, conv_b, w_down_bf, x1, gpost, st, *, batch, seqlen, tm=512, tn=256):
    m = hf.shape[0]
    nj = D_FF // tn
    km1 = FFN_CONV - 1
    prompt = st is None
    if prompt:
        nseq, lseq = 1, tm
        tps = seqlen // tm
        nstate = m // tm
    else:
        lseq = seqlen
        nseq = tm // lseq
        tps = 1
        nstate = batch
    cur = lambda j: jnp.minimum(j, nj - 1)
    prv = lambda j: jnp.maximum(j - 1, 0)
    in_specs = [
        pl.BlockSpec((tm, D_MODEL), lambda i, j: (i, 0)),
        pl.BlockSpec((D_MODEL, tn), lambda i, j: (0, cur(j))),
        pl.BlockSpec((D_MODEL, tn), lambda i, j: (0, nj + cur(j))),
        pl.BlockSpec((FFN_CONV, tn), lambda i, j: (0, prv(j))),
        pl.BlockSpec((FFN_CONV, tn), lambda i, j: (0, nj + prv(j))),
        pl.BlockSpec((1, tn), lambda i, j: (0, prv(j))),
        pl.BlockSpec((1, tn), lambda i, j: (0, nj + prv(j))),
        pl.BlockSpec((tn, D_MODEL), lambda i, j: (prv(j), 0)),
        pl.BlockSpec((tm, D_MODEL), lambda i, j: (i, 0), pipeline_mode=pl.Buffered(1)),
        pl.BlockSpec((1, D_MODEL), lambda i, j: (0, 0)),
    ]
    args = [hf, w_up_bf, w_up_bf, conv_w, conv_w, conv_b, conv_b, w_down_bf, x1, gpost]
    scratch = [pltpu.VMEM((nseq, lseq + SUBLANES, tn), F32), pltpu.VMEM((nseq, lseq + SUBLANES, tn), F32),
               pltpu.VMEM((tm, tn), F32), pltpu.VMEM((tm, tn), F32)]
    if prompt:
        scratch += [pltpu.VMEM((SUBLANES, D_FF), F32), pltpu.VMEM((SUBLANES, D_FF), F32)]
    else:
        in_specs += [pl.BlockSpec((nseq, km1, tn), lambda i, j: (i, 0, cur(j))),
                     pl.BlockSpec((nseq, km1, tn), lambda i, j: (i, 0, nj + cur(j)))]
        args += [st, st]
    return pl.pallas_call(
        functools.partial(_ffn_kernel, nseq=nseq, lseq=lseq, tiles_per_seq=tps, tn=tn, nj=nj),
        grid=(m // tm, nj + 1),
        in_specs=in_specs,
        out_specs=[pl.BlockSpec((tm, D_MODEL), lambda i, j: (i, 0)),
                   pl.BlockSpec((nseq, km1, tn), lambda i, j: (i, 0, cur(j))),
                   pl.BlockSpec((nseq, km1, tn), lambda i, j: (i, 0, cur(j)))],
        out_shape=[jax.ShapeDtypeStruct((m, D_MODEL), F32),
                   jax.ShapeDtypeStruct((nstate, km1, D_FF), F32),
                   jax.ShapeDtypeStruct((nstate, km1, D_FF), F32)],
        scratch_shapes=scratch,
        compiler_params=_cparams(("arbitrary", "arbitrary")),
        name="ffn_prompt" if prompt else "ffn_sample",
    )(*args)


def _layer(x, states, w):
    b, l, _ = x.shape
    m = b * l
    x2d = x.reshape(m, D_MODEL)
    proj, dt_raw = _inproj(x2d, w["g_mix_pre"], w["w_in"], w["w_dt"])
    nstate = S5_G * S5_P
    if states is None:
        ya, s5r, s5i = _s5(proj, None, None, w["s5"], batch=b, seqlen=l)
        gper = SSD_GROUPS
        dtg = dt_raw.reshape(1, m, SSD_HEADS)
        yb, cnx, cnb, cnc, hn = _ssd(proj, dtg, None, None, w["ssd"], batch=b, seqlen=l, gper=gper)
        fconv0 = None
    else:
        s5r0, s5i0, conv0, h0, fconv0 = states
        ya, s5r, s5i = _s5(proj, s5r0.reshape(b, nstate), s5i0.reshape(b, nstate), w["s5"], batch=b, seqlen=l)
        gper = SSD_GROUPS // 2
        dtg = dt_raw.reshape(m, SSD_GROUPS // gper, SSD_HPG * gper).transpose(1, 0, 2)
        yb, cnx, cnb, cnc, hn = _ssd(proj, dtg, conv0, h0, w["ssd"], batch=b, seqlen=l, gper=gper)
    x1, hf = _outproj(ya, yb, w["w_out"], x2d, w["g_mix_post"], w["g_ffn_pre"])
    y, ng, nv = _ffn(hf, w["w_up"], w["ffn_conv_w"], w["ffn_conv_b"], w["w_down"], x1, w["g_ffn_post"], fconv0,
                     batch=b, seqlen=l)
    if states is None:
        tps = ng.shape[0] // b
        ng = ng[tps - 1::tps]
        nv = nv[tps - 1::tps]
    return (y.reshape(b, l, D_MODEL),
            s5r.reshape(1, b, S5_G, S5_P), s5i.reshape(1, b, S5_G, S5_P),
            jnp.concatenate([cnx, cnb, cnc], axis=-1)[None], hn[None],
            jnp.concatenate([ng, nv], axis=-1)[None])


def kernel(x_prompt, x_sample, state_s5_re, state_s5_im, state_ssd_conv, state_ssd, state_ffn_conv,
           g_mix_pre, g_mix_post, g_ffn_pre, g_ffn_post, w_in,
           s5_a_re, s5_a_im, s5_log_dt, s5_b_re, s5_b_im, s5_c_re, s5_c_im, s5_d, s5_w_glu, s5_b_glu,
           ssd_conv_w, ssd_conv_b, ssd_dt_bias, ssd_a_log, ssd_d, ssd_norm_w,
           w_out, ffn_w_up, ffn_conv_w, ffn_conv_b, ffn_w_down):
    w_in_bf = w_in[0].astype(BF16)
    rep = lambda v: jnp.repeat(v.astype(F32), SSD_HEADDIM).reshape(1, D_SSD)
    w = dict(
        g_mix_pre=g_mix_pre[0].reshape(1, D_MODEL), g_mix_post=g_mix_post[0].reshape(1, D_MODEL),
        g_ffn_pre=g_ffn_pre[0].reshape(1, D_MODEL), g_ffn_post=g_ffn_post[0].reshape(1, D_MODEL),
        w_in=w_in_bf, w_dt=w_in_bf[:, D_PROJ:],
        s5=_s5_weights(s5_a_re[0], s5_a_im[0], s5_log_dt[0], s5_b_re[0], s5_b_im[0], s5_c_re[0], s5_c_im[0],
                       s5_d[0], s5_w_glu[0], s5_b_glu[0]),
        ssd=dict(conv_w=ssd_conv_w[0], conv_b=ssd_conv_b[0].reshape(1, D_XBC),
                 a=rep(-jnp.exp(ssd_a_log[0].astype(F32))), dtb=rep(ssd_dt_bias[0]), d=rep(ssd_d[0]),
                 nw=ssd_norm_w[0].reshape(1, D_SSD)),
        w_out=w_out[0].astype(BF16), w_up=ffn_w_up[0].astype(BF16),
        ffn_conv_w=ffn_conv_w[0], ffn_conv_b=ffn_conv_b[0].reshape(1, 2 * D_FF),
        w_down=ffn_w_down[0].astype(BF16),
    )
    outs_p = _layer(x_prompt, None, w)
    outs_s = _layer(x_sample, (state_s5_re[0], state_s5_im[0], state_ssd_conv[0], state_ssd[0], state_ffn_conv[0]), w)
    return (outs_p[0], outs_s[0]) + outs_p[1:] + outs_s[1:]
```
